```python
import math
import jax, jax.numpy as jnp
from jax import lax
import numpy as np

D_MODEL = 1024
BATCH = 4
SEQ = 4096
DEPTH = 4

HEAD_DIM = 64
ROPE_THETA = 10000.0
A_Q_HEADS = 6
A_KV_HEADS = 2
A_GROUP = A_Q_HEADS // A_KV_HEADS
A_HALF_WINDOW = 128
DIL_GROUPS = ((128, 1), (512, 4), (2048, 16))
N_DIL = len(DIL_GROUPS)
B_HEADS_PER_GROUP = 2
B_HEADS = B_HEADS_PER_GROUP * N_DIL
C_HEADS = 4
GRID_W = 64
NA_ROWS = 8
NA_COLS = 16
N_BRANCH = 3
N_EXPERTS = 16
EXPERT_FF = 2048
EC_CAPACITY = 2
RMS_EPS = 1e-6
NEG_INF = -1e30

A_Q_W = A_Q_HEADS * HEAD_DIM
A_KV_W = A_KV_HEADS * HEAD_DIM
B_W = B_HEADS * HEAD_DIM
B_OUT_W = B_HEADS_PER_GROUP * HEAD_DIM
C_W = C_HEADS * HEAD_DIM
PROJ_WIDTHS = (A_Q_W, A_KV_W, A_KV_W, B_W, B_W, B_W, C_W, C_W, C_W, N_BRANCH * D_MODEL)
PROJ_DIM = sum(PROJ_WIDTHS)

kernel_name = 'hybrid_gated_sparse_attn_ec_moe_encoder'


def _rmsnorm(x, g):
    xf = x.astype(jnp.float32)
    y = xf * lax.rsqrt(jnp.mean(xf * xf, axis=-1, keepdims=True) + RMS_EPS)
    return (y * g.astype(jnp.float32)).astype(x.dtype)


def _rope(x):
    s, hd = x.shape[1], x.shape[-1]
    half = hd // 2
    inv_freq = ROPE_THETA ** (-jnp.arange(half, dtype=jnp.float32) * (2.0 / hd))
    ang = jnp.arange(s, dtype=jnp.float32)[:, None] * inv_freq[None, :]
    cos = jnp.cos(ang)[None, :, None, :]
    sin = jnp.sin(ang)[None, :, None, :]
    xf = x.astype(jnp.float32)
    x1, x2 = xf[..., :half], xf[..., half:]
    return jnp.concatenate([x1 * cos - x2 * sin, x2 * cos + x1 * sin], axis=-1).astype(x.dtype)


def _key_windows(t, nb, blk, lp):
    n, L, h, hd = t.shape
    tp = jnp.pad(t, ((0, 0), (blk, lp - L + blk), (0, 0), (0, 0))).reshape(n, nb + 2, blk, h, hd)
    return jnp.concatenate([tp[:, :-2], tp[:, 1:-1], tp[:, 2:]], axis=2)


def _banded_attn(q, k, v, half_w, sink=None):
    n, L, hk, g, hd = q.shape
    blk = half_w
    nb = -(-L // blk)
    lp = nb * blk
    qb = jnp.pad(q, ((0, 0), (0, lp - L), (0, 0), (0, 0), (0, 0))).reshape(n, nb, blk, hk, g, hd)
    kw = _key_windows(k, nb, blk, lp)
    vw = _key_windows(v, nb, blk, lp)
    s = jnp.einsum('nbqhgd,nbkhd->nbhgqk', qb, kw, preferred_element_type=jnp.float32) * (hd ** -0.5)
    qpos = jnp.arange(nb)[:, None] * blk + jnp.arange(blk)[None, :]
    kpos = jnp.arange(nb)[:, None] * blk - blk + jnp.arange(3 * blk)[None, :]
    rel = kpos[:, None, :] - qpos[:, :, None]
    valid = (jnp.abs(rel) <= half_w) & (kpos[:, None, :] >= 0) & (kpos[:, None, :] < L)
    s = jnp.where(valid[None, :, None, None], s, NEG_INF)
    m = jnp.max(s, axis=-1)
    if sink is not None:
        sk = sink.astype(jnp.float32)[None, None, :, :, None]
        m = jnp.maximum(m, sk)
    p = jnp.exp(s - m[..., None])
    den = jnp.sum(p, axis=-1)
    if sink is not None:
        den = den + jnp.exp(sk - m)
    o = jnp.einsum('nbhgqk,nbkhd->nbqhgd', p.astype(v.dtype), vw, preferred_element_type=jnp.float32)
    o = o / jnp.moveaxis(den, -1, 2)[..., None]
    o = o.reshape(n, lp, hk, g, hd)[:, :L].astype(q.dtype)
    lse = jnp.moveaxis(m + jnp.log(den), -1, 2).reshape(n, lp, hk, g)[:, :L]
    return o, lse


def _dilated_group(q, k, v, dil, half_w):
    b, s, h, hd = q.shape
    L = s // dil

    def to_res(t):
        return t.reshape(b, L, dil, h, hd).transpose(0, 2, 1, 3, 4).reshape(b * dil, L, h, hd)

    o, lse = _banded_attn(to_res(q)[:, :, :, None, :], to_res(k), to_res(v), half_w)
    o = o[:, :, :, 0].reshape(b, dil, L, h, hd).transpose(0, 2, 1, 3, 4).reshape(b, s, h, hd)
    lse = lse[..., 0].reshape(b, dil, L, h).transpose(0, 2, 1, 3).reshape(b, s, h)
    return o, lse


def _dilated_mixture(q, k, v):
    outs, lses = [], []
    for gi, (w, d) in enumerate(DIL_GROUPS):
        o, lse = _dilated_group(q[:, :, gi], k[:, :, gi], v[:, :, gi], d, w // (2 * d))
        outs.append(o)
        lses.append(lse)
    alpha = jax.nn.softmax(jnp.stack(lses, axis=0), axis=0)
    o = jnp.sum(alpha[..., None] * jnp.stack(outs, axis=0).astype(jnp.float32), axis=0)
    b, s = q.shape[0], q.shape[1]
    return o.reshape(b, s, B_OUT_W).astype(q.dtype)


def _neighbourhood_attn(q, k, v, rpb):
    b, s, h, hd = q.shape
    rows = s // GRID_W
    kr = min(NA_ROWS, rows)
    kc = NA_COLS
    r = jnp.arange(rows)
    start_r = jnp.clip(r - kr // 2, 0, rows - kr)
    row_idx = start_r[:, None] + jnp.arange(kr)[None, :]
    c = jnp.arange(GRID_W)
    start_c = jnp.clip(c - kc // 2, 0, GRID_W - kc)
    col_ok = (c[None, :] >= start_c[:, None]) & (c[None, :] < start_c[:, None] + kc)
    qg = q.reshape(b, rows, GRID_W, h, hd)
    kg = k.reshape(b, rows, GRID_W, h, hd)[:, row_idx]
    vg = v.reshape(b, rows, GRID_W, h, hd)[:, row_idx]
    sc = jnp.einsum('brqhd,brjkhd->bhrqjk', qg, kg, preferred_element_type=jnp.float32) * (hd ** -0.5)
    dr_idx = row_idx - r[:, None] + (NA_ROWS - 1)
    dc_idx = jnp.clip(c[None, :] - c[:, None] + (NA_COLS - 1), 0, 2 * NA_COLS - 2)
    bias = rpb.astype(jnp.float32)[:, dr_idx][:, :, :, dc_idx]
    sc = sc + bias.transpose(0, 1, 3, 2, 4)[None]
    sc = jnp.where(col_ok[:, None, :], sc, NEG_INF)
    p = jax.nn.softmax(sc.reshape(b, h, rows, GRID_W, kr * GRID_W), axis=-1)
    p = p.reshape(b, h, rows, GRID_W, kr, GRID_W).astype(v.dtype)
    o = jnp.einsum('bhrqjk,brjkhd->brqhd', p, vg, preferred_element_type=jnp.float32)
    return o.reshape(b, s, h * hd).astype(q.dtype)


def _ec_moe(xn, w_router, w_gate, w_up, w_down):
    b, s, d = xn.shape
    cap = EC_CAPACITY * s // N_EXPERTS
    aff = jax.nn.softmax(jnp.einsum('bsd,de->bse', xn, w_router, preferred_element_type=jnp.float32), axis=-1)
    vals, idx = lax.top_k(jnp.swapaxes(aff, 1, 2), cap)
    bi = jnp.arange(b)[:, None, None]
    xe = xn[bi, idx]
    hid = jax.nn.silu(jnp.einsum('becd,edf->becf', xe, w_gate)) * jnp.einsum('becd,edf->becf', xe, w_up)
    ye = jnp.einsum('becf,efd->becd', hid, w_down) * vals[..., None].astype(xn.dtype)
    return jnp.zeros_like(xn).at[bi, idx].add(ye)


def setup_inputs(seed: int = 0) -> dict:
    key = jax.random.key(seed)
    ks = jax.random.split(key, 16)
    f32 = jnp.float32
    nrm = jax.random.normal
    return {
        'x': nrm(ks[0], (BATCH, SEQ, D_MODEL), f32),
        'w_in': nrm(ks[1], (DEPTH, D_MODEL, PROJ_DIM), f32) * D_MODEL ** -0.5,
        'b_gate': nrm(ks[2], (DEPTH, N_BRANCH * D_MODEL), f32) * 0.1,
        'sink': nrm(ks[3], (DEPTH, A_Q_HEADS), f32) * 0.5,
        'rpb': nrm(ks[4], (DEPTH, C_HEADS, 2 * NA_ROWS - 1, 2 * NA_COLS - 1), f32) * 0.1,
        'w_a': nrm(ks[5], (DEPTH, A_Q_W, D_MODEL), f32) * A_Q_W ** -0.5,
        'w_b': nrm(ks[6], (DEPTH, B_OUT_W, D_MODEL), f32) * B_OUT_W ** -0.5,
        'w_c': nrm(ks[7], (DEPTH, C_W, D_MODEL), f32) * C_W ** -0.5,
        'w_out': nrm(ks[8], (DEPTH, D_MODEL, D_MODEL), f32) * D_MODEL ** -0.5,
        'norm_mix': 1.0 + 0.05 * nrm(ks[9], (DEPTH, D_MODEL), f32),
        'norm_ffn': 1.0 + 0.05 * nrm(ks[10], (DEPTH, D_MODEL), f32),
        'w_router': nrm(ks[11], (DEPTH, D_MODEL, N_EXPERTS), f32) * D_MODEL ** -0.5,
        'w_gate_e': nrm(ks[12], (DEPTH, N_EXPERTS, D_MODEL, EXPERT_FF), f32) * D_MODEL ** -0.5,
        'w_up_e': nrm(ks[13], (DEPTH, N_EXPERTS, D_MODEL, EXPERT_FF), f32) * D_MODEL ** -0.5,
        'w_down_e': nrm(ks[14], (DEPTH, N_EXPERTS, EXPERT_FF, D_MODEL), f32) * EXPERT_FF ** -0.5,
        'norm_final': 1.0 + 0.05 * nrm(ks[15], (D_MODEL,), f32),
    }


def reference(x, w_in, b_gate, sink, rpb, w_a, w_b, w_c, w_out, norm_mix, norm_ffn,
              w_router, w_gate_e, w_up_e, w_down_e, norm_final):
    b, s = x.shape[0], x.shape[1]
    splits = [int(i) for i in np.cumsum(PROJ_WIDTHS)[:-1]]
    for l in range(DEPTH):
        xn = _rmsnorm(x, norm_mix[l])
        proj = jnp.einsum('bsd,dp->bsp', xn, w_in[l])
        qa, ka, va, qb, kb, vb, qc, kc, vc, gl = jnp.split(proj, splits, axis=-1)
        qa = _rope(qa.reshape(b, s, A_Q_HEADS, HEAD_DIM)).reshape(b, s, A_KV_HEADS, A_GROUP, HEAD_DIM)
        ka = _rope(ka.reshape(b, s, A_KV_HEADS, HEAD_DIM))
        va = va.reshape(b, s, A_KV_HEADS, HEAD_DIM)
        oa = _banded_attn(qa, ka, va, A_HALF_WINDOW, sink[l].reshape(A_KV_HEADS, A_GROUP))[0]
        oa = oa.reshape(b, s, A_Q_W)
        qb = _rope(qb.reshape(b, s, B_HEADS, HEAD_DIM)).reshape(b, s, N_DIL, B_HEADS_PER_GROUP, HEAD_DIM)
        kb = _rope(kb.reshape(b, s, B_HEADS, HEAD_DIM)).reshape(b, s, N_DIL, B_HEADS_PER_GROUP, HEAD_DIM)
        vb = vb.reshape(b, s, N_DIL, B_HEADS_PER_GROUP, HEAD_DIM)
        ob = _dilated_mixture(qb, kb, vb)
        oc = _neighbourhood_attn(qc.reshape(b, s, C_HEADS, HEAD_DIM), kc.reshape(b, s, C_HEADS, HEAD_DIM),
                                 vc.reshape(b, s, C_HEADS, HEAD_DIM), rpb[l])
        gates = jax.nn.sigmoid((gl + b_gate[l]).astype(jnp.float32)).astype(x.dtype)
        gates = gates.reshape(b, s, N_BRANCH, D_MODEL)
        merged = (gates[:, :, 0] * (oa @ w_a[l]) + gates[:, :, 1] * (ob @ w_b[l])
                  + gates[:, :, 2] * (oc @ w_c[l]))
        x = x + merged @ w_out[l]
        x = x + _ec_moe(_rmsnorm(x, norm_ffn[l]), w_router[l], w_gate_e[l], w_up_e[l], w_down_e[l])
    return _rmsnorm(x, norm_final)
```

```python
import functools

import numpy as np
import jax
import jax.numpy as jnp
from jax import lax
from jax.experimental import pallas as pl
from jax.experimental.pallas import tpu as pltpu

F32 = jnp.float32
BF16 = jnp.bfloat16

D_MODEL = 1024
HEAD_DIM = 64
ROPE_THETA = 10000.0
A_Q_HEADS = 6
A_HALF_WINDOW = 128
DIL_GROUPS = ((128, 1), (512, 4), (2048, 16))
C_HEADS = 4
GRID_W = 64
NA_ROWS = 8
NA_COLS = 16
N_EXPERTS = 16
EXPERT_FF = 2048
EC_CAPACITY = 2
RMS_EPS = 1e-6
NEG_INF = -1e30

A_Q_W = 384
A_KV_W = 128
B_W = 384
C_W = 256
QKV_W = A_Q_W + 2 * A_KV_W + 3 * B_W + 3 * C_W
LANES = 128

VMEM_LIMIT = 56 * 1024 * 1024


def _cparams(sem):
    return pltpu.CompilerParams(dimension_semantics=sem, vmem_limit_bytes=VMEM_LIMIT)


def _rms(x, g):
    return (x * lax.rsqrt(jnp.mean(x * x, axis=-1, keepdims=True) + RMS_EPS)) * g


def _dot_nt(a, b):
    return lax.dot_general(a, b, (((1,), (1,)), ((), ())), preferred_element_type=F32)


def _inproj_kernel(x_ref, g_ref, w_ref, cos_ref, sa_ref, sb_ref,
                   qa_ref, ka_ref, va_ref, qb_ref, kb_ref, vb_ref, qc_ref, kc_ref, vc_ref):
    xn = _rms(x_ref[...], g_ref[...]).astype(BF16)
    cos = cos_ref[...]
    sa = sa_ref[...]
    sb = sb_ref[...]
    scale = HEAD_DIM ** -0.5

    def proj(c0, width):
        return jnp.dot(xn, w_ref[:, c0:c0 + width], preferred_element_type=F32)

    def rope(y):
        outs = []
        for j in range(y.shape[1] // LANES):
            yj = y[:, j * LANES:(j + 1) * LANES]
            outs.append(yj * cos + pltpu.roll(yj, 96, 1) * sa + pltpu.roll(yj, 32, 1) * sb)
        return outs[0] if len(outs) == 1 else jnp.concatenate(outs, axis=1)

    def with_swapped(y):
        return jnp.concatenate([y, pltpu.roll(y, 64, 1)], axis=1)

    c = 0
    qa_ref[...] = (rope(proj(c, A_Q_W)) * scale).astype(BF16); c += A_Q_W
    ka_ref[...] = with_swapped(rope(proj(c, A_KV_W))).astype(BF16); c += A_KV_W
    va_ref[...] = with_swapped(proj(c, A_KV_W)).astype(BF16); c += A_KV_W
    qb_ref[...] = rope(proj(c, B_W)) * scale; c += B_W
    kb_ref[...] = rope(proj(c, B_W)); c += B_W
    vb_ref[...] = proj(c, B_W); c += B_W
    qc_ref[...] = (proj(c, C_W) * scale).astype(BF16); c += C_W
    kc_ref[...] = proj(c, C_W).astype(BF16); c += C_W
    vc_ref[...] = proj(c, C_W).astype(BF16)


def _inproj(x2, g, w, cos, sa, sb, *, seq, tm):
    t = x2.shape[0]
    nt = t // tm
    npos = seq // tm
    row = lambda i: (i, 0)
    pos = lambda i: (i % npos, 0)
    const = lambda i: (0, 0)
    widths_dt = [(A_Q_W, BF16), (2 * A_KV_W, BF16), (2 * A_KV_W, BF16),
                 (B_W, F32), (B_W, F32), (B_W, F32),
                 (C_W, BF16), (C_W, BF16), (C_W, BF16)]
    return pl.pallas_call(
        _inproj_kernel,
        grid=(nt,),
        in_specs=[pl.BlockSpec((tm, D_MODEL), row),
                  pl.BlockSpec((1, D_MODEL), const),
                  pl.BlockSpec((D_MODEL, QKV_W), const),
                  pl.BlockSpec((tm, LANES), pos),
                  pl.BlockSpec((tm, LANES), pos),
                  pl.BlockSpec((tm, LANES), pos)],
        out_specs=[pl.BlockSpec((tm, w_), row) for w_, _ in widths_dt],
        out_shape=[jax.ShapeDtypeStruct((t, w_), dt) for w_, dt in widths_dt],
        compiler_params=_cparams(("arbitrary",)),
        name="inproj",
    )(x2, g, w, cos, sa, sb)


def _softmax_pv(s, vv, extra_logit=None):
    m = jnp.max(s, axis=-1, keepdims=True)
    if extra_logit is not None:
        m = jnp.maximum(m, extra_logit)
    p = jnp.exp(s - m)
    den = jnp.sum(p, axis=-1, keepdims=True)
    if extra_logit is not None:
        den = den + jnp.exp(extra_logit - m)
    o = jnp.dot(p.astype(BF16), vv, preferred_element_type=F32)
    return o, m, den


def _mixa_kernel(sink_ref, q_ref, k_ref, v_ref, o_ref, *, tq, win, seq):
    q0 = pl.program_id(1) * tq
    start = pl.multiple_of(jnp.clip(q0 - A_HALF_WINDOW, 0, seq - win), LANES)
    k = k_ref[pl.ds(start, win), :]
    v = v_ref[pl.ds(start, win), :]
    rel = ((start - q0) + lax.broadcasted_iota(jnp.int32, (tq, win), 1)
           - lax.broadcasted_iota(jnp.int32, (tq, win), 0))
    bias = jnp.where(jnp.abs(rel) <= A_HALF_WINDOW, 0.0, NEG_INF).astype(F32)
    lane = lax.broadcasted_iota(jnp.int32, (tq, LANES), 1)
    outs = []
    for p in range(A_Q_HEADS // 2):
        qp = q_ref[:, p * LANES:(p + 1) * LANES]
        acc = None
        for half in range(2):
            h = 2 * p + half
            kv_head = h // (A_Q_HEADS // 2)
            hm = (lane >= HEAD_DIM) if half else (lane < HEAD_DIM)
            qm = jnp.where(hm, qp, jnp.zeros_like(qp))
            c0 = 0 if kv_head == half else LANES
            s = _dot_nt(qm, k[:, c0:c0 + LANES]) + bias
            o, m, den = _softmax_pv(s, v[:, c0:c0 + LANES], sink_ref[h])
            o = o / den
            acc = o if acc is None else jnp.where(hm, o, acc)
        outs.append(acc)
    o_ref[...] = jnp.concatenate(outs, axis=1).astype(o_ref.dtype)


def _mixa(sink, qa, ka, va, *, tq=256):
    b, seq, _ = qa.shape
    win = tq + 2 * A_HALF_WINDOW
    kern = functools.partial(_mixa_kernel, tq=tq, win=win, seq=seq)
    return pl.pallas_call(
        kern,
        grid=(b, seq // tq),
        in_specs=[pl.BlockSpec(memory_space=pltpu.SMEM),
                  pl.BlockSpec((None, tq, A_Q_W), lambda bi, i: (bi, i, 0)),
                  pl.BlockSpec((None, seq, 2 * A_KV_W), lambda bi, i: (bi, 0, 0)),
                  pl.BlockSpec((None, seq, 2 * A_KV_W), lambda bi, i: (bi, 0, 0))],
        out_specs=pl.BlockSpec((None, tq, A_Q_W), lambda bi, i: (bi, i, 0)),
        out_shape=jax.ShapeDtypeStruct((b, seq, A_Q_W), BF16),
        compiler_params=_cparams(("arbitrary", "arbitrary")),
        name="mixer_a",
    )(sink, qa, ka, va)


def _mixb_kernel(q_ref, k_ref, v_ref, o_ref, lse_ref, *, d, seq, tq, hw):
    sub = seq // d
    win = tq + 2 * hw
    nblk = sub // tq
    lane = lax.broadcasted_iota(jnp.int32, (tq, LANES), 1)
    iq = lax.broadcasted_iota(jnp.int32, (tq, win), 0)
    ik = lax.broadcasted_iota(jnp.int32, (tq, win), 1)

    def rows(start, n):
        return pl.ds(pl.multiple_of(start, 8), n) if d == 1 else pl.ds(start, n, stride=d)

    def body(it, carry):
        r = it // nblk
        l0 = (it % nblk) * tq
        ks = jnp.clip(l0 - hw, 0, sub - win)
        q = q_ref[rows(r + d * l0, tq), :]
        kk = k_ref[rows(r + d * ks, win), :].astype(BF16)
        vv = v_ref[rows(r + d * ks, win), :].astype(BF16)
        rel = (ks - l0) + ik - iq
        bias = jnp.where(jnp.abs(rel) <= hw, 0.0, NEG_INF).astype(F32)
        acc_o = None
        acc_l = None
        for half in range(2):
            hm = (lane >= HEAD_DIM) if half else (lane < HEAD_DIM)
            qm = jnp.where(hm, q, 0.0).astype(BF16)
            s = _dot_nt(qm, kk) + bias
            o, m, den = _softmax_pv(s, vv)
            o = o / den
            lse = jnp.broadcast_to(m + jnp.log(den), (tq, LANES))
            acc_o = o if acc_o is None else jnp.where(hm, o, acc_o)
            acc_l = lse if acc_l is None else jnp.where(hm, lse, acc_l)
        o_ref[rows(r + d * l0, tq), :] = acc_o
        lse_ref[rows(r + d * l0, tq), :] = acc_l
        return carry

    lax.fori_loop(0, d * nblk, body, 0)


def _mixb_group(qb, kb, vb, gi, *, tq=128):
    b, seq, _ = qb.shape
    w, d = DIL_GROUPS[gi]
    hw = w // (2 * d)
    assert seq // d >= tq + 2 * hw
    kern = functools.partial(_mixb_kernel, d=d, seq=seq, tq=tq, hw=hw)
    spec = pl.BlockSpec((None, seq, LANES), lambda bi: (bi, 0, gi))
    ospec = pl.BlockSpec((None, seq, LANES), lambda bi: (bi, 0, 0))
    return pl.pallas_call(
        kern,
        grid=(b,),
        in_specs=[spec, spec, spec],
        out_specs=[ospec, ospec],
        out_shape=[jax.ShapeDtypeStruct((b, seq, LANES), F32)] * 2,
        compiler_params=_cparams(("arbitrary",)),
        name=f"mixer_b{gi}",
    )(qb, kb, vb)


C_QROWS = 4
C_WROWS = C_QROWS + NA_ROWS


def _mixc_kernel(q_ref, k_ref, v_ref, bias_ref, o_ref, *, rows):
    r0 = pl.program_id(1) * C_QROWS
    ws = jnp.clip(r0 - NA_ROWS // 2, 0, rows - C_WROWS)
    st = pl.multiple_of(ws * GRID_W, GRID_W)
    nk = C_WROWS * GRID_W
    nq = C_QROWS * GRID_W
    k = k_ref[pl.ds(st, nk), :]
    v = v_ref[pl.ds(st, nk), :]
    lane = lax.broadcasted_iota(jnp.int32, (nq, LANES), 1)
    outs = []
    for p in range(C_HEADS // 2):
        qp = q_ref[:, p * LANES:(p + 1) * LANES]
        acc = None
        for half in range(2):
            h = 2 * p + half
            hm = (lane >= HEAD_DIM) if half else (lane < HEAD_DIM)
            qm = jnp.where(hm, qp, jnp.zeros_like(qp))
            s = _dot_nt(qm, k[:, p * LANES:(p + 1) * LANES]) + bias_ref[h]
            o, m, den = _softmax_pv(s, v[:, p * LANES:(p + 1) * LANES])
            o = o / den
            acc = o if acc is None else jnp.where(hm, o, acc)
        outs.append(acc)
    o_ref[...] = jnp.concatenate(outs, axis=1).astype(o_ref.dtype)


def _mixc_bias_index(rows):
    nsteps = rows // C_QROWS
    variants = [0, 1, nsteps - 1]
    c = np.arange(GRID_W)
    start_c = np.clip(c - NA_COLS // 2, 0, GRID_W - NA_COLS)
    col_ok = (c[None, :] >= start_c[:, None]) & (c[None, :] < start_c[:, None] + NA_COLS)
    dc = np.clip(c[None, :] - c[:, None] + (NA_COLS - 1), 0, 2 * NA_COLS - 2)
    kr_n = min(NA_ROWS, rows)
    dr_all, ok_all = [], []
    for step in variants:
        r0 = step * C_QROWS
        ws = int(np.clip(r0 - NA_ROWS // 2, 0, rows - C_WROWS))
        r = r0 + np.arange(C_QROWS)
        kr = ws + np.arange(C_WROWS)
        sr = np.clip(r - kr_n // 2, 0, rows - kr_n)
        row_ok = (kr[None, :] >= sr[:, None]) & (kr[None, :] < sr[:, None] + kr_n)
        dr = np.clip(kr[None, :] - r[:, None] + (NA_ROWS - 1), 0, 2 * NA_ROWS - 2)
        dr_full = np.broadcast_to(dr[:, None, :, None], (C_QROWS, GRID_W, C_WROWS, GRID_W))
        ok = row_ok[:, None, :, None] & col_ok[None, :, None, :]
        dr_all.append(dr_full.reshape(C_QROWS * GRID_W, C_WROWS * GRID_W))
        ok_all.append(ok.reshape(C_QROWS * GRID_W, C_WROWS * GRID_W))
    dc_full = np.broadcast_to(dc[None, :, None, :], (C_QROWS, GRID_W, C_WROWS, GRID_W))
    dc_full = dc_full.reshape(C_QROWS * GRID_W, C_WROWS * GRID_W)
    return np.stack(dr_all), dc_full, np.stack(ok_all)


def _mixc_bias(rpb_l, rows):
    dr, dc, ok = _mixc_bias_index(rows)
    tbl = rpb_l.astype(F32)[:, dr, dc[None]]
    tbl = jnp.where(ok[None], tbl, NEG_INF)
    return jnp.transpose(tbl, (1, 0, 2, 3))


def _mixc(qc, kc, vc, bias):
    b, seq, _ = qc.shape
    rows = seq // GRID_W
    nsteps = rows // C_QROWS
    nq = C_QROWS * GRID_W
    nk = C_WROWS * GRID_W

    def bias_map(bi, i):
        return (jnp.where(i == 0, 0, jnp.where(i == nsteps - 1, 2, 1)), 0, 0, 0)

    return pl.pallas_call(
        functools.partial(_mixc_kernel, rows=rows),
        grid=(b, nsteps),
        in_specs=[pl.BlockSpec((None, nq, C_W), lambda bi, i: (bi, i, 0)),
                  pl.BlockSpec((None, seq, C_W), lambda bi, i: (bi, 0, 0)),
                  pl.BlockSpec((None, seq, C_W), lambda bi, i: (bi, 0, 0)),
                  pl.BlockSpec((None, C_HEADS, nq, nk), bias_map)],
        out_specs=pl.BlockSpec((None, nq, C_W), lambda bi, i: (bi, i, 0)),
        out_shape=jax.ShapeDtypeStruct((b, seq, C_W), BF16),
        compiler_params=_cparams(("arbitrary", "arbitrary")),
        name="mixer_c",
    )(qc, kc, vc, bias)


def _merge_kernel(x_ref, oa_ref, ob_ref, lse_ref, oc_ref, gm_ref, wg_ref, bg_ref,
                  wa_ref, wb_ref, wc_ref, wo_ref, gf_ref, wr_ref,
                  h_ref, xn2_ref, aff_ref):
    x = x_ref[...]
    xn = _rms(x, gm_ref[...]).astype(BF16)
    lse = lse_ref[...]
    mx = jnp.max(lse, axis=0)
    e = jnp.exp(lse - mx[None])
    alpha = e / jnp.sum(e, axis=0)[None]
    ob = jnp.sum(alpha * ob_ref[...], axis=0).astype(BF16)
    branches = ((oa_ref[...], wa_ref), (ob, wb_ref), (oc_ref[...], wc_ref))
    merged = None
    for j, (o, w_ref) in enumerate(branches):
        gl = jnp.dot(xn, wg_ref[:, j * D_MODEL:(j + 1) * D_MODEL], preferred_element_type=F32)
        gate = jax.nn.sigmoid(gl + bg_ref[:, j * D_MODEL:(j + 1) * D_MODEL])
        term = gate * jnp.dot(o, w_ref[...], preferred_element_type=F32)
        merged = term if merged is None else merged + term
    h = x + jnp.dot(merged.astype(BF16), wo_ref[...], preferred_element_type=F32)
    h_ref[...] = h
    xn2 = _rms(h, gf_ref[...])
    xn2_ref[...] = xn2
    logits = _dot_nt(wr_ref[...], xn2.astype(BF16))
    mz = jnp.max(logits, axis=0, keepdims=True)
    ez = jnp.exp(logits - mz)
    aff_ref[...] = ez / jnp.sum(ez, axis=0, keepdims=True)


def _merge(x2, oa, obp, lsep, oc, gm, wg, bg, wa, wb, wc, wo, gf, wr, *, seq, tm):
    t = x2.shape[0]
    nt = t // tm
    per_b = seq // tm
    row = lambda i: (i, 0)
    const = lambda i: (0, 0)
    n_dil = len(DIL_GROUPS)
    return pl.pallas_call(
        _merge_kernel,
        grid=(nt,),
        in_specs=[pl.BlockSpec((tm, D_MODEL), row),
                  pl.BlockSpec((tm, A_Q_W), row),
                  pl.BlockSpec((n_dil, tm, LANES), lambda i: (0, i, 0)),
                  pl.BlockSpec((n_dil, tm, LANES), lambda i: (0, i, 0)),
                  pl.BlockSpec((tm, C_W), row),
                  pl.BlockSpec((1, D_MODEL), const),
                  pl.BlockSpec((D_MODEL, 3 * D_MODEL), const),
                  pl.BlockSpec((1, 3 * D_MODEL), const),
                  pl.BlockSpec((A_Q_W, D_MODEL), const),
                  pl.BlockSpec((LANES, D_MODEL), const),
                  pl.BlockSpec((C_W, D_MODEL), const),
                  pl.BlockSpec((D_MODEL, D_MODEL), const),
                  pl.BlockSpec((1, D_MODEL), const),
                  pl.BlockSpec((N_EXPERTS, D_MODEL), const)],
        out_specs=[pl.BlockSpec((tm, D_MODEL), row),
                   pl.BlockSpec((tm, D_MODEL), row),
                   pl.BlockSpec((None, N_EXPERTS, tm), lambda i: (i // per_b, 0, i % per_b))],
        out_shape=[jax.ShapeDtypeStruct((t, D_MODEL), F32),
                   jax.ShapeDtypeStruct((t, D_MODEL), F32),
                   jax.ShapeDtypeStruct((t // seq, N_EXPERTS, seq), F32)],
        compiler_params=_cparams(("arbitrary",)),
        name="merge_router",
    )(x2, oa, obp, lsep, oc, gm, wg, bg, wa, wb, wc, wo, gf, wr)


def _moe_kernel(idx_ref, xn_hbm, vals_ref, wg_ref, wu_ref, wd_ref, y_ref,
                xe_ref, stage_ref, sem, *, nb, cap, seq):
    e = pl.program_id(0)
    f = pl.program_id(1)
    nf = pl.num_programs(1)

    def row_copy(src_row, dst_row):
        return pltpu.make_async_copy(xn_hbm.at[pl.ds(src_row, 1), :],
                                     stage_ref.at[pl.ds(dst_row, 1), :], sem)

    @pl.when(f == 0)
    def _gather():
        for b in range(nb):
            base = (e * nb + b) * cap

            def issue(c, carry):
                row_copy(idx_ref[base + c] + b * seq, c).start()
                return carry

            lax.fori_loop(0, cap, issue, 0)

            def drain(c, carry):
                row_copy(0, c).wait()
                return carry

            lax.fori_loop(0, cap, drain, 0)
            xe_ref[b * cap:(b + 1) * cap, :] = stage_ref[...].astype(BF16)

    wg = wg_ref[...].astype(BF16)
    wu = wu_ref[...].astype(BF16)
    wd = wd_ref[...].astype(BF16)
    for b in range(nb):
        sl = slice(b * cap, (b + 1) * cap)
        xe = xe_ref[sl, :]
        g = jnp.dot(xe, wg, preferred_element_type=F32)
        u = jnp.dot(xe, wu, preferred_element_type=F32)
        hid = (g * jax.nn.sigmoid(g) * u).astype(BF16)
        y = jnp.dot(hid, wd, preferred_element_type=F32)

        @pl.when(f == 0)
        def _first():
            y_ref[sl, :] = y

        @pl.when(f > 0)
        def _rest():
            y_ref[sl, :] += y

    @pl.when(f == nf - 1)
    def _scale():
        y_ref[...] = y_ref[...] * vals_ref[...]


def _moe(idx_flat, xn2, vals_col, wg, wu, wd, *, nb, cap, seq, fc=512):
    ne = wg.shape[0]
    ff = wg.shape[2]
    rows = nb * cap
    kern = functools.partial(_moe_kernel, nb=nb, cap=cap, seq=seq)
    grid_spec = pltpu.PrefetchScalarGridSpec(
        num_scalar_prefetch=1,
        grid=(ne, ff // fc),
        in_specs=[pl.BlockSpec(memory_space=pl.ANY),
                  pl.BlockSpec((None, rows, 1), lambda e, f, idx: (e, 0, 0)),
                  pl.BlockSpec((None, D_MODEL, fc), lambda e, f, idx: (e, 0, f)),
                  pl.BlockSpec((None, D_MODEL, fc), lambda e, f, idx: (e, 0, f)),
                  pl.BlockSpec((None, fc, D_MODEL), lambda e, f, idx: (e, f, 0))],
        out_specs=pl.BlockSpec((None, rows, D_MODEL), lambda e, f, idx: (e, 0, 0)),
        scratch_shapes=[pltpu.VMEM((rows, D_MODEL), BF16),
                        pltpu.VMEM((cap, D_MODEL), F32),
                        pltpu.SemaphoreType.DMA(())],
    )
    return pl.pallas_call(
        kern,
        grid_spec=grid_spec,
        out_shape=jax.ShapeDtypeStruct((ne, rows, D_MODEL), F32),
        compiler_params=_cparams(("arbitrary", "arbitrary")),
        name="moe_experts",
    )(idx_flat, xn2, vals_col, wg, wu, wd)


def _combine_kernel(idx_ref, h_ref, y_ref, o_ref, *, seq, cap, chunk):
    e = pl.program_id(1)

    @pl.when(e == 0)
    def _init():
        o_ref[...] = h_ref[...]

    y = y_ref[...]
    y_hi = y.astype(BF16)
    y_lo = (y - y_hi.astype(F32)).astype(BF16)
    idx = idx_ref[...]
    for ci in range(seq // chunk):
        tok = ci * chunk + lax.broadcasted_iota(jnp.int32, (chunk, cap), 0)
        onehot = jnp.where(tok == idx, 1.0, 0.0).astype(BF16)
        o_ref[ci * chunk:(ci + 1) * chunk, :] += (
            jnp.dot(onehot, y_hi, preferred_element_type=F32)
            + jnp.dot(onehot, y_lo, preferred_element_type=F32))


def _combine(idx4, h3, y4):
    b, seq, _ = h3.shape
    ne, _, cap, _ = y4.shape
    return pl.pallas_call(
        functools.partial(_combine_kernel, seq=seq, cap=cap, chunk=512),
        grid=(b, ne),
        in_specs=[pl.BlockSpec((None, None, 1, cap), lambda bi, e: (bi, e, 0, 0)),
                  pl.BlockSpec((None, seq, D_MODEL), lambda bi, e: (bi, 0, 0),
                               pipeline_mode=pl.Buffered(1)),
                  pl.BlockSpec((None, None, cap, D_MODEL), lambda bi, e: (e, bi, 0, 0))],
        out_specs=pl.BlockSpec((None, seq, D_MODEL), lambda bi, e: (bi, 0, 0),
                               pipeline_mode=pl.Buffered(1)),
        out_shape=jax.ShapeDtypeStruct((b, seq, D_MODEL), F32),
        compiler_params=_cparams(("arbitrary", "arbitrary")),
        name="moe_combine",
    )(idx4, h3, y4)


def _final_kernel(x_ref, g_ref, o_ref):
    o_ref[...] = _rms(x_ref[...], g_ref[...])


def _final_norm(x2, g, *, tm=1024):
    t = x2.shape[0]
    return pl.pallas_call(
        _final_kernel,
        grid=(t // tm,),
        in_specs=[pl.BlockSpec((tm, D_MODEL), lambda i: (i, 0)),
                  pl.BlockSpec((1, D_MODEL), lambda i: (0, 0))],
        out_specs=pl.BlockSpec((tm, D_MODEL), lambda i: (i, 0)),
        out_shape=jax.ShapeDtypeStruct((t, D_MODEL), F32),
        compiler_params=_cparams(("arbitrary",)),
        name="final_norm",
    )(x2, g)


def _rope_tables(seq):
    half = HEAD_DIM // 2
    inv_freq = ROPE_THETA ** (-jnp.arange(half, dtype=F32) * (2.0 / HEAD_DIM))
    ang = jnp.arange(seq, dtype=F32)[:, None] * inv_freq[None, :]
    cos = jnp.cos(ang)
    sin = jnp.sin(ang)
    zero = jnp.zeros_like(sin)
    cos_t = jnp.tile(cos, (1, LANES // half))
    sa = jnp.tile(jnp.concatenate([-sin, zero], axis=1), (1, LANES // HEAD_DIM))
    sb = jnp.tile(jnp.concatenate([zero, sin], axis=1), (1, LANES // HEAD_DIM))
    return cos_t, sa, sb


def kernel(x, w_in, b_gate, sink, rpb, w_a, w_b, w_c, w_out, norm_mix, norm_ffn,
           w_router, w_gate_e, w_up_e, w_down_e, norm_final):
    b, seq, dm = x.shape
    depth = w_in.shape[0]
    t = b * seq
    cap = EC_CAPACITY * seq // N_EXPERTS
    tm = 512
    cos_t, sa, sb = _rope_tables(seq)
    w_qkv = w_in[:, :, :QKV_W].astype(BF16)
    w_gl = w_in[:, :, QKV_W:].astype(BF16)
    w_a16, w_b16, w_c16, w_o16 = (w.astype(BF16) for w in (w_a, w_b, w_c, w_out))
    w_rt = jnp.swapaxes(w_router, 1, 2).astype(BF16)
    x2 = x.reshape(t, dm)
    for l in range(depth):
        qa, ka, va, qb, kb, vb, qc, kc, vc = _inproj(
            x2, norm_mix[l][None], w_qkv[l], cos_t, sa, sb, seq=seq, tm=tm)
        r3 = lambda a: a.reshape(b, seq, a.shape[-1])
        oa = _mixa(sink[l], r3(qa), r3(ka), r3(va))
        ob_parts = [_mixb_group(r3(qb), r3(kb), r3(vb), gi) for gi in range(len(DIL_GROUPS))]
        obp = jnp.stack([p[0] for p in ob_parts]).reshape(len(DIL_GROUPS), t, LANES)
        lsep = jnp.stack([p[1] for p in ob_parts]).reshape(len(DIL_GROUPS), t, LANES)
        oc = _mixc(r3(qc), r3(kc), r3(vc), _mixc_bias(rpb[l], seq // GRID_W))
        h2, xn2, aff_t = _merge(
            x2, oa.reshape(t, A_Q_W), obp, lsep, oc.reshape(t, C_W), norm_mix[l][None],
            w_gl[l], b_gate[l][None], w_a16[l], w_b16[l], w_c16[l], w_o16[l],
            norm_ffn[l][None], w_rt[l], seq=seq, tm=tm)
        vals, idx = lax.top_k(aff_t, cap)
        idx_eb = jnp.swapaxes(idx, 0, 1)
        vals_col = jnp.swapaxes(vals, 0, 1).reshape(N_EXPERTS, b * cap, 1)
        y = _moe(idx_eb.reshape(-1), xn2, vals_col, w_gate_e[l], w_up_e[l], w_down_e[l],
                 nb=b, cap=cap, seq=seq)
        x3 = _combine(idx.reshape(b, N_EXPERTS, 1, cap), h2.reshape(b, seq, dm),
                      y.reshape(N_EXPERTS, b, cap, dm))
        x2 = x3.reshape(t, dm)
    return _final_norm(x2, norm_final[None]).reshape(b, seq, dm)
```
